```python
import jax, jax.numpy as jnp
from jax import lax
import numpy as np

D_MODEL = 1024
BATCH = 1
SEQ = 16384
DEPTH = 1

D_MIX = D_MODEL
D_POOL = D_MIX // 2
POOL_WINDOWS = (2, 4, 8, 16)
N_POOL_GROUPS = len(POOL_WINDOWS)
POOL_GROUP_DIM = D_POOL // N_POOL_GROUPS
D_ATTN = D_MIX - D_POOL
HEAD_DIM = 64
N_HEADS = D_ATTN // HEAD_DIM
Q_BLOCK = 128
D_FF = ((8 * D_MODEL // 3 + 127) // 128) * 128
CONV_WIDTH = 3
EPS = 1e-6
D_IN_PROJ = D_POOL + 3 * D_ATTN

kernel_name = "hymba_pool_stickbreaking_convffn"


def rms_normalize(x):
    xf = x.astype(jnp.float32)
    return xf * lax.rsqrt(jnp.mean(xf * xf, axis=-1, keepdims=True) + EPS)


def rmsnorm(x, gain):
    return (rms_normalize(x) * gain.astype(jnp.float32)).astype(x.dtype)


def multiscale_pool(u, w_pool):
    B, S, _ = u.shape
    uf = u.astype(jnp.float32)
    cs = jnp.pad(jnp.cumsum(uf, axis=1), ((0, 0), (1, 0), (0, 0)))
    t1 = jnp.arange(1, S + 1)
    groups = []
    for g, w in enumerate(POOL_WINDOWS):
        sl = slice(g * POOL_GROUP_DIM, (g + 1) * POOL_GROUP_DIM)
        c = cs[:, :, sl]
        lagged = jnp.pad(c[:, :S + 1 - w], ((0, 0), (w - 1, 0), (0, 0)))
        cnt = jnp.minimum(t1, w).astype(jnp.float32)[None, :, None]
        groups.append((c[:, 1:] - lagged) / cnt - uf[:, :, sl])
    p = jnp.stack(groups, axis=2)
    y = jnp.einsum('bsgc,gcd->bsgd', p, w_pool.astype(jnp.float32))
    return y.reshape(B, S, D_POOL)


def stick_breaking_attention(q, k, v):
    B, H, S, Dh = q.shape
    nb = S // Q_BLOCK
    qb = q.astype(jnp.float32).reshape(B, H, nb, Q_BLOCK, Dh).transpose(2, 0, 1, 3, 4)
    kf = k.astype(jnp.float32)
    vf = v.astype(jnp.float32)
    key_pos = jnp.arange(S)
    scale = Dh ** -0.5

    def block(args):
        i, qi = args
        z = jnp.einsum('bhqd,bhkd->bhqk', qi, kf) * scale
        q_pos = i * Q_BLOCK + jnp.arange(Q_BLOCK)
        causal = key_pos[None, :] < q_pos[:, None]
        log_1m_beta = jnp.where(causal, jax.nn.log_sigmoid(-z), 0.0)
        stick = lax.cumsum(log_1m_beta, axis=3, reverse=True) - log_1m_beta
        a = jnp.where(causal, jnp.exp(jax.nn.log_sigmoid(z) + stick), 0.0)
        return jnp.einsum('bhqk,bhkd->bhqd', a, vf)

    out = lax.map(block, (jnp.arange(nb), qb))
    return out.transpose(1, 2, 0, 3, 4).reshape(B, H, S, Dh)


def causal_depthwise_conv(h, conv_w, conv_b):
    S = h.shape[1]
    hp = jnp.pad(h, ((0, 0), (CONV_WIDTH - 1, 0), (0, 0)))
    y = conv_b
    for kk in range(CONV_WIDTH):
        y = y + conv_w[kk] * hp[:, kk:kk + S]
    return y


def setup_inputs(seed: int = 0) -> dict:
    key = jax.random.key(seed)
    ks = jax.random.split(key, 20)
    f32 = jnp.float32

    def nrm(k, shape, scale):
        return jax.random.normal(k, shape, f32) * scale

    def gain(k, n):
        return 1.0 + 0.05 * jax.random.normal(k, (DEPTH, n), f32)

    return {
        "x": jax.random.normal(ks[0], (BATCH, SEQ, D_MODEL), f32),
        "norm_mix_pre": gain(ks[1], D_MODEL),
        "w_in": nrm(ks[2], (DEPTH, D_MODEL, D_IN_PROJ), D_MODEL ** -0.5),
        "w_pool": nrm(ks[3], (DEPTH, N_POOL_GROUPS, POOL_GROUP_DIM, POOL_GROUP_DIM), POOL_GROUP_DIM ** -0.5),
        "pool_scale": gain(ks[4], D_POOL),
        "attn_scale": gain(ks[5], D_ATTN),
        "w_out": nrm(ks[6], (DEPTH, D_MIX, D_MODEL), D_MIX ** -0.5),
        "norm_mix_post": gain(ks[7], D_MODEL),
        "norm_ffn_pre": gain(ks[8], D_MODEL),
        "w_up": nrm(ks[9], (DEPTH, D_MODEL, 2 * D_FF), D_MODEL ** -0.5),
        "conv_w": nrm(ks[10], (DEPTH, CONV_WIDTH, 2 * D_FF), CONV_WIDTH ** -0.5),
        "conv_b": nrm(ks[11], (DEPTH, 2 * D_FF), 0.01),
        "w_down": nrm(ks[12], (DEPTH, D_FF, D_MODEL), D_FF ** -0.5),
        "norm_ffn_post": gain(ks[13], D_MODEL),
    }


def reference(x, norm_mix_pre, w_in, w_pool, pool_scale, attn_scale, w_out, norm_mix_post,
              norm_ffn_pre, w_up, conv_w, conv_b, w_down, norm_ffn_post):
    B, S, _ = x.shape
    for l in range(DEPTH):
        h = rmsnorm(x, norm_mix_pre[l])
        proj = h @ w_in[l]
        u_pool = proj[..., :D_POOL]
        q = proj[..., D_POOL:D_POOL + D_ATTN]
        k = proj[..., D_POOL + D_ATTN:D_POOL + 2 * D_ATTN]
        v = proj[..., D_POOL + 2 * D_ATTN:]

        pool_out = multiscale_pool(u_pool, w_pool[l])

        def heads(t):
            return t.reshape(B, S, N_HEADS, HEAD_DIM).transpose(0, 2, 1, 3)
        attn_out = stick_breaking_attention(heads(q), heads(k), heads(v))
        attn_out = attn_out.transpose(0, 2, 1, 3).reshape(B, S, D_ATTN)

        merged = jnp.concatenate([
            rms_normalize(pool_out) * pool_scale[l].astype(jnp.float32),
            rms_normalize(attn_out) * attn_scale[l].astype(jnp.float32),
        ], axis=-1).astype(x.dtype)
        mix = merged @ w_out[l]
        x = x + rmsnorm(mix, norm_mix_post[l])

        h = rmsnorm(x, norm_ffn_pre[l])
        up = causal_depthwise_conv(h @ w_up[l], conv_w[l], conv_b[l])
        gate, val = up[..., :D_FF], up[..., D_FF:]
        f = (jax.nn.silu(gate) * val) @ w_down[l]
        x = x + rmsnorm(f, norm_ffn_post[l])
    return x
```

```python
import functools

import jax
import jax.numpy as jnp
from jax import lax
from jax.experimental import pallas as pl
from jax.experimental.pallas import tpu as pltpu

EPS = 1e-6
POOL_WINDOWS = (2, 4, 8, 16)
HEAD_DIM = 64
CONV_WIDTH = 3
LANES = 128
SUBLANES = 8
POOL_HALO = 16
VMEM_LIMIT_BYTES = 56 * 1024 * 1024

BF16 = jnp.bfloat16
F32 = jnp.float32


def _rms_scale(x):
    return lax.rsqrt(jnp.mean(x * x, axis=-1, keepdims=True) + EPS)


def _dot(a, b):
    return jnp.dot(a, b, preferred_element_type=F32)


def _in_proj_kernel(x_ref, g_ref, win_ref, wpool_ref, ps_ref,
                    mp_ref, q_ref, k_ref, v_ref, halo_ref, *, tm, d_pool, d_attn):
    i = pl.program_id(0)

    @pl.when(i == 0)
    def _():
        halo_ref[...] = jnp.zeros_like(halo_ref)

    x = x_ref[...]
    hb = (x * _rms_scale(x) * g_ref[...]).astype(BF16)

    u = _dot(hb, win_ref[:, 0:d_pool])
    ext = jnp.concatenate([halo_ref[...], u], axis=0)
    halo_ref[...] = u[tm - POOL_HALO:, :]

    t1 = i * tm + lax.broadcasted_iota(jnp.int32, (tm, 1), 0) + 1
    gd = d_pool // len(POOL_WINDOWS)
    ys = []
    for g, w in enumerate(POOL_WINDOWS):
        s = ext[:, g * gd:(g + 1) * gd]
        sh = 1
        while sh < w:
            s = s + pltpu.roll(s, sh, axis=0)
            sh *= 2
        inv_cnt = 1.0 / jnp.minimum(t1, w).astype(F32)
        p = s[POOL_HALO:, :] * inv_cnt - u[:, g * gd:(g + 1) * gd]
        ys.append(_dot(p.astype(BF16), wpool_ref[g]))
    y = jnp.concatenate(ys, axis=1)
    mp_ref[...] = (y * _rms_scale(y) * ps_ref[...]).astype(BF16)

    scale = HEAD_DIM ** -0.5
    q_ref[...] = (_dot(hb, win_ref[:, d_pool:d_pool + d_attn]) * scale).astype(BF16)
    k_ref[...] = _dot(hb, win_ref[:, d_pool + d_attn:d_pool + 2 * d_attn]).astype(BF16)
    v_ref[...] = _dot(hb, win_ref[:, d_pool + 2 * d_attn:d_pool + 3 * d_attn]).astype(BF16)


def _in_proj(x, g_pre, w_in, w_pool, pool_scale, *, tm):
    s, d = x.shape
    d_pool = pool_scale.shape[-1]
    d_attn = (w_in.shape[1] - d_pool) // 3
    row = lambda i: (i, 0)
    const2 = lambda i: (0, 0)
    kern = functools.partial(_in_proj_kernel, tm=tm, d_pool=d_pool, d_attn=d_attn)
    return pl.pallas_call(
        kern,
        grid=(s // tm,),
        in_specs=[
            pl.BlockSpec((tm, d), row),
            pl.BlockSpec((1, d), const2),
            pl.BlockSpec(w_in.shape, const2),
            pl.BlockSpec(w_pool.shape, lambda i: (0, 0, 0)),
            pl.BlockSpec((1, d_pool), const2),
        ],
        out_specs=[
            pl.BlockSpec((tm, d_pool), row),
            pl.BlockSpec((tm, d_attn), row),
            pl.BlockSpec((tm, d_attn), row),
            pl.BlockSpec((tm, d_attn), row),
        ],
        out_shape=[
            jax.ShapeDtypeStruct((s, d_pool), BF16),
            jax.ShapeDtypeStruct((s, d_attn), BF16),
            jax.ShapeDtypeStruct((s, d_attn), BF16),
            jax.ShapeDtypeStruct((s, d_attn), BF16),
        ],
        scratch_shapes=[pltpu.VMEM((POOL_HALO, d_pool), F32)],
        compiler_params=pltpu.CompilerParams(
            dimension_semantics=("arbitrary",), vmem_limit_bytes=VMEM_LIMIT_BYTES),
        name="in_proj",
    )(x, g_pre, w_in, w_pool, pool_scale)


def _attn_kernel(q_ref, k_ref, v_ref, o_ref, acc_ref, *, tq):
    qi = pl.program_id(1)
    tk = tq
    q = q_ref[...]
    lane_q = lax.broadcasted_iota(jnp.int32, (tq, LANES), 1)
    zero_q = jnp.zeros_like(q)
    q_heads = (jnp.where(lane_q < HEAD_DIM, q, zero_q), jnp.where(lane_q >= HEAD_DIM, q, zero_q))

    r_i = lax.broadcasted_iota(jnp.int32, (tk, tk), 0)
    c_i = lax.broadcasted_iota(jnp.int32, (tk, tk), 1)
    tri = jnp.where(r_i > c_i, 1.0, 0.0).astype(BF16)
    causal = c_i < r_i
    lane_v = lax.broadcasted_iota(jnp.int32, (tk, LANES), 1)

    acc_ref[...] = jnp.zeros_like(acc_ref)

    def block(j, carries, masked):
        start = pl.multiple_of(j * tk, tk)
        kb = k_ref[pl.ds(start, tk), :]
        vb = v_ref[pl.ds(start, tk), :]
        zero_v = jnp.zeros_like(vb)
        v_heads = (jnp.where(lane_v < HEAD_DIM, vb, zero_v), jnp.where(lane_v >= HEAD_DIM, vb, zero_v))
        new_carries = []
        upd = None
        for qh, vh, carry in zip(q_heads, v_heads, carries):
            z = lax.dot_general(qh, kb, (((1,), (1,)), ((), ())), preferred_element_type=F32)
            sp = jnp.maximum(z, 0.0) + jnp.log(1.0 + jnp.exp(-jnp.abs(z)))
            sp_sum = jnp.where(causal, sp, 0.0) if masked else sp
            hi = sp_sum.astype(BF16)
            lo = (sp_sum - hi.astype(F32)).astype(BF16)
            cs = _dot(hi, tri) + _dot(lo, tri)
            a = jnp.exp(z - sp - cs - carry)
            if masked:
                a = jnp.where(causal, a, 0.0)
            pv = _dot(a.astype(BF16), vh)
            upd = pv if upd is None else upd + pv
            new_carries.append(carry + jnp.sum(sp_sum, axis=-1, keepdims=True))
        acc_ref[...] += upd
        return tuple(new_carries)

    zero_c = jnp.zeros((tq, 1), F32)
    carries = block(qi, (zero_c, zero_c), True)
    lax.fori_loop(0, qi, lambda t, c: block(qi - 1 - t, c, False), carries)
    o_ref[...] = acc_ref[...]


def _attention(q, k, v, *, tq):
    s, d_attn = q.shape
    n_pairs = d_attn // LANES
    kern = functools.partial(_attn_kernel, tq=tq)
    return pl.pallas_call(
        kern,
        grid=(n_pairs, s // tq),
        in_specs=[
            pl.BlockSpec((tq, LANES), lambda h, i: (i, h)),
            pl.BlockSpec((s, LANES), lambda h, i: (0, h)),
            pl.BlockSpec((s, LANES), lambda h, i: (0, h)),
        ],
        out_specs=pl.BlockSpec((tq, LANES), lambda h, i: (i, h)),
        out_shape=jax.ShapeDtypeStruct((s, d_attn), F32),
        scratch_shapes=[pltpu.VMEM((tq, LANES), F32)],
        compiler_params=pltpu.CompilerParams(
            dimension_semantics=("arbitrary", "arbitrary"), vmem_limit_bytes=VMEM_LIMIT_BYTES),
        name="attn",
    )(q, k, v)


def _out_proj_kernel(mp_ref, at_ref, x_ref, wout_ref, as_ref, gpost_ref, gffn_ref,
                     x1_ref, h2_ref, *, d_pool):
    at = at_ref[...]
    an = (at * _rms_scale(at) * as_ref[...]).astype(BF16)
    mix = _dot(mp_ref[...], wout_ref[0:d_pool, :]) + _dot(an, wout_ref[d_pool:, :])
    x1 = x_ref[...] + mix * _rms_scale(mix) * gpost_ref[...]
    x1_ref[...] = x1
    h2_ref[...] = (x1 * _rms_scale(x1) * gffn_ref[...]).astype(BF16)


def _out_proj(mp, attn, x, w_out, attn_scale, g_post, g_ffn, *, tm):
    s, d = x.shape
    d_pool = mp.shape[1]
    d_attn = attn.shape[1]
    row = lambda i: (i, 0)
    const2 = lambda i: (0, 0)
    kern = functools.partial(_out_proj_kernel, d_pool=d_pool)
    return pl.pallas_call(
        kern,
        grid=(s // tm,),
        in_specs=[
            pl.BlockSpec((tm, d_pool), row),
            pl.BlockSpec((tm, d_attn), row),
            pl.BlockSpec((tm, d), row),
            pl.BlockSpec(w_out.shape, const2),
            pl.BlockSpec((1, d_attn), const2),
            pl.BlockSpec((1, d), const2),
            pl.BlockSpec((1, d), const2),
        ],
        out_specs=[pl.BlockSpec((tm, d), row), pl.BlockSpec((tm, d), row)],
        out_shape=[jax.ShapeDtypeStruct((s, d), F32), jax.ShapeDtypeStruct((s, d), BF16)],
        compiler_params=pltpu.CompilerParams(
            dimension_semantics=("arbitrary",), vmem_limit_bytes=VMEM_LIMIT_BYTES),
        name="out_proj",
    )(mp, attn, x, w_out, attn_scale, g_post, g_ffn)


def _causal_conv(raw, prev, cw, cb):
    tm = raw.shape[0]
    row8 = lax.broadcasted_iota(jnp.int32, (SUBLANES, raw.shape[1]), 0)

    def shifted(sh):
        r = pltpu.roll(raw, sh, axis=0)
        head = jnp.where(row8 < sh, pltpu.roll(prev, sh, axis=0), r[0:SUBLANES, :])
        return jnp.concatenate([head, r[SUBLANES:, :]], axis=0)

    return cb + cw[0:1, :] * shifted(2) + cw[1:2, :] * shifted(1) + cw[2:3, :] * raw


def _ffn_kernel(h2_ref, x1_ref, wg_ref, wv_ref, cwg_ref, cwv_ref, cbg_ref, cbv_ref, wd_ref, gpost_ref,
                o_ref, acc_ref, carry_ref, *, tm, fc):
    i = pl.program_id(0)
    c = pl.program_id(1)
    nc = pl.num_programs(1)

    @pl.when(c == 0)
    def _():
        acc_ref[...] = jnp.zeros_like(acc_ref)

    h2 = h2_ref[...]
    g_raw = _dot(h2, wg_ref[...])
    v_raw = _dot(h2, wv_ref[...])
    prev = jnp.where(i == 0, 0.0, carry_ref[c])
    carry_ref[c] = jnp.concatenate([g_raw[tm - SUBLANES:, :], v_raw[tm - SUBLANES:, :]], axis=1)

    gate = _causal_conv(g_raw, prev[:, 0:fc], cwg_ref[...], cbg_ref[...])
    val = _causal_conv(v_raw, prev[:, fc:], cwv_ref[...], cbv_ref[...])
    act = gate * (1.0 / (1.0 + jnp.exp(-gate))) * val
    acc_ref[...] += _dot(act.astype(BF16), wd_ref[...])

    @pl.when(c == nc - 1)
    def _():
        f = acc_ref[...]
        o_ref[...] = x1_ref[...] + f * _rms_scale(f) * gpost_ref[...]


def _ffn(h2, x1, w_up, conv_w, conv_b, w_down, g_post, *, tm, fc):
    s, d = x1.shape
    d_ff = w_down.shape[0]
    nc = d_ff // fc
    kern = functools.partial(_ffn_kernel, tm=tm, fc=fc)
    return pl.pallas_call(
        kern,
        grid=(s // tm, nc),
        in_specs=[
            pl.BlockSpec((tm, d), lambda i, c: (i, 0)),
            pl.BlockSpec((tm, d), lambda i, c: (i, 0)),
            pl.BlockSpec((d, fc), lambda i, c: (0, c)),
            pl.BlockSpec((d, fc), lambda i, c: (0, c + nc)),
            pl.BlockSpec((CONV_WIDTH, fc), lambda i, c: (0, c)),
            pl.BlockSpec((CONV_WIDTH, fc), lambda i, c: (0, c + nc)),
            pl.BlockSpec((1, fc), lambda i, c: (0, c)),
            pl.BlockSpec((1, fc), lambda i, c: (0, c + nc)),
            pl.BlockSpec((fc, d), lambda i, c: (c, 0)),
            pl.BlockSpec((1, d), lambda i, c: (0, 0)),
        ],
        out_specs=pl.BlockSpec((tm, d), lambda i, c: (i, 0)),
        out_shape=jax.ShapeDtypeStruct((s, d), F32),
        scratch_shapes=[
            pltpu.VMEM((tm, d), F32),
            pltpu.VMEM((nc, SUBLANES, 2 * fc), F32),
        ],
        compiler_params=pltpu.CompilerParams(
            dimension_semantics=("arbitrary", "arbitrary"), vmem_limit_bytes=VMEM_LIMIT_BYTES),
        name="ffn",
    )(h2, x1, w_up, w_up, conv_w, conv_w, conv_b, conv_b, w_down, g_post)


def _pick_tile(s, pref):
    t = min(pref, s)
    assert s % t == 0, (s, t)
    return t


def kernel(x, norm_mix_pre, w_in, w_pool, pool_scale, attn_scale, w_out, norm_mix_post, norm_ffn_pre,
           w_up, conv_w, conv_b, w_down, norm_ffn_post):
    b, s, d = x.shape
    depth = w_in.shape[0]
    assert b == 1
    d_ff = w_down.shape[1]
    fc = 2 * LANES
    assert d_ff % fc == 0 and w_up.shape[2] == 2 * d_ff
    xs = x[0]
    for l in range(depth):
        row1 = lambda a: a[l][None, :]
        mp, q, k, v = _in_proj(xs, row1(norm_mix_pre), w_in[l].astype(BF16), w_pool[l].astype(BF16),
                               row1(pool_scale), tm=_pick_tile(s, 512))
        attn = _attention(q, k, v, tq=_pick_tile(s, 256))
        x1, h2 = _out_proj(mp, attn, xs, w_out[l].astype(BF16), row1(attn_scale), row1(norm_mix_post),
                           row1(norm_ffn_pre), tm=_pick_tile(s, 512))
        xs = _ffn(h2, x1, w_up[l].astype(BF16), conv_w[l], row1(conv_b), w_down[l].astype(BF16),
                  row1(norm_ffn_post), tm=_pick_tile(s, 1024), fc=fc)
    return xs[None]
```

```python
import functools

import jax
import jax.numpy as jnp
from jax import lax
from jax.experimental import pallas as pl
from jax.experimental.pallas import tpu as pltpu

EPS = 1e-6
POOL_WINDOWS = (2, 4, 8, 16)
HEAD_DIM = 64
CONV_WIDTH = 3
LANES = 128
SUBLANES = 8
POOL_HALO = 16
EXP_UNDERFLOW = 105.0
VMEM_LIMIT_BYTES = 56 * 1024 * 1024

BF16 = jnp.bfloat16
F32 = jnp.float32


def _rms_scale(x):
    return lax.rsqrt(jnp.mean(x * x, axis=-1, keepdims=True) + EPS)


def _dot(a, b):
    return jnp.dot(a, b, preferred_element_type=F32)


def _in_proj_kernel(x_ref, g_ref, win_ref, wpool_ref, ps_ref,
                    mp_ref, q_ref, k_ref, v_ref, halo_ref, *, tm, d_pool, d_attn):
    i = pl.program_id(0)

    @pl.when(i == 0)
    def _():
        halo_ref[...] = jnp.zeros_like(halo_ref)

    x = x_ref[...]
    hb = (x * _rms_scale(x) * g_ref[...]).astype(BF16)

    u = _dot(hb, win_ref[:, 0:d_pool])
    ext = jnp.concatenate([halo_ref[...], u], axis=0)
    halo_ref[...] = u[tm - POOL_HALO:, :]

    t1 = i * tm + lax.broadcasted_iota(jnp.int32, (tm, 1), 0) + 1
    gd = d_pool // len(POOL_WINDOWS)
    ys = []
    for g, w in enumerate(POOL_WINDOWS):
        s = ext[:, g * gd:(g + 1) * gd]
        sh = 1
        while sh < w:
            s = s + pltpu.roll(s, sh, axis=0)
            sh *= 2
        inv_cnt = 1.0 / jnp.minimum(t1, w).astype(F32)
        p = s[POOL_HALO:, :] * inv_cnt - u[:, g * gd:(g + 1) * gd]
        ys.append(_dot(p.astype(BF16), wpool_ref[g]))
    y = jnp.concatenate(ys, axis=1)
    mp_ref[...] = (y * _rms_scale(y) * ps_ref[...]).astype(BF16)

    scale = HEAD_DIM ** -0.5
    q_ref[...] = (_dot(hb, win_ref[:, d_pool:d_pool + d_attn]) * scale).astype(BF16)
    k_ref[...] = _dot(hb, win_ref[:, d_pool + d_attn:d_pool + 2 * d_attn]).astype(BF16)
    v_ref[...] = _dot(hb, win_ref[:, d_pool + 2 * d_attn:d_pool + 3 * d_attn]).astype(BF16)


def _in_proj(x, g_pre, w_in, w_pool, pool_scale, *, tm):
    s, d = x.shape
    d_pool = pool_scale.shape[-1]
    d_attn = (w_in.shape[1] - d_pool) // 3
    row = lambda i: (i, 0)
    const2 = lambda i: (0, 0)
    kern = functools.partial(_in_proj_kernel, tm=tm, d_pool=d_pool, d_attn=d_attn)
    return pl.pallas_call(
        kern,
        grid=(s // tm,),
        in_specs=[
            pl.BlockSpec((tm, d), row),
            pl.BlockSpec((1, d), const2),
            pl.BlockSpec(w_in.shape, const2),
            pl.BlockSpec(w_pool.shape, lambda i: (0, 0, 0)),
            pl.BlockSpec((1, d_pool), const2),
        ],
        out_specs=[
            pl.BlockSpec((tm, d_pool), row),
            pl.BlockSpec((tm, d_attn), row),
            pl.BlockSpec((tm, d_attn), row),
            pl.BlockSpec((tm, d_attn), row),
        ],
        out_shape=[
            jax.ShapeDtypeStruct((s, d_pool), BF16),
            jax.ShapeDtypeStruct((s, d_attn), BF16),
            jax.ShapeDtypeStruct((s, d_attn), BF16),
            jax.ShapeDtypeStruct((s, d_attn), BF16),
        ],
        scratch_shapes=[pltpu.VMEM((POOL_HALO, d_pool), F32)],
        compiler_params=pltpu.CompilerParams(
            dimension_semantics=("arbitrary",), vmem_limit_bytes=VMEM_LIMIT_BYTES),
        name="in_proj",
    )(x, g_pre, w_in, w_pool, pool_scale)


def _attn_kernel(q_ref, k_ref, v_ref, o_ref, acc_ref, *, tq):
    qi = pl.program_id(1)
    tk = tq
    q = q_ref[...]
    lane_q = lax.broadcasted_iota(jnp.int32, (tq, LANES), 1)
    zero_q = jnp.zeros_like(q)
    q_heads = (jnp.where(lane_q < HEAD_DIM, q, zero_q), jnp.where(lane_q >= HEAD_DIM, q, zero_q))

    r_i = lax.broadcasted_iota(jnp.int32, (tk, tk), 0)
    c_i = lax.broadcasted_iota(jnp.int32, (tk, tk), 1)
    tri = jnp.where(r_i >= c_i, 1.0, 0.0).astype(BF16)
    tri2 = jnp.concatenate([tri, tri], axis=0)
    causal = c_i < r_i
    lane_v = lax.broadcasted_iota(jnp.int32, (tk, LANES), 1)

    acc_ref[...] = jnp.zeros_like(acc_ref)

    def block(j, carries, masked):
        start = pl.multiple_of(j * tk, tk)
        kb = k_ref[pl.ds(start, tk), :]
        vb = v_ref[pl.ds(start, tk), :]
        zero_v = jnp.zeros_like(vb)
        v_heads = (jnp.where(lane_v < HEAD_DIM, vb, zero_v), jnp.where(lane_v >= HEAD_DIM, vb, zero_v))
        new_carries = []
        upd = None
        for qh, vh, carry in zip(q_heads, v_heads, carries):
            z = lax.dot_general(qh, kb, (((1,), (1,)), ((), ())), preferred_element_type=F32)
            sp = jnp.maximum(z, 0.0) + jnp.log(1.0 + jnp.exp(-jnp.abs(z)))
            if masked:
                sp = jnp.where(causal, sp, 0.0)
            hi = sp.astype(BF16)
            lo = (sp - hi.astype(F32)).astype(BF16)
            cs = _dot(jnp.concatenate([hi, lo], axis=1), tri2)
            a = jnp.exp(z - cs - carry)
            if masked:
                a = jnp.where(causal, a, 0.0)
            pv = _dot(a.astype(BF16), vh)
            upd = pv if upd is None else upd + pv
            new_carries.append(carry + jnp.sum(sp, axis=-1, keepdims=True))
        acc_ref[...] += upd
        return tuple(new_carries)

    zero_c = jnp.zeros((tq, 1), F32)
    c0, c1 = block(qi, (zero_c, zero_c), True)

    def unfinished(state):
        j, c0, c1 = state
        return jnp.logical_and(j >= 0, jnp.min(jnp.minimum(c0, c1)) < EXP_UNDERFLOW)

    def step(state):
        j, c0, c1 = state
        c0, c1 = block(j, (c0, c1), False)
        return j - 1, c0, c1

    lax.while_loop(unfinished, step, (qi - 1, c0, c1))
    o_ref[...] = acc_ref[...]


def _attention(q, k, v, *, tq):
    s, d_attn = q.shape
    n_pairs = d_attn // LANES
    kern = functools.partial(_attn_kernel, tq=tq)
    return pl.pallas_call(
        kern,
        grid=(n_pairs, s // tq),
        in_specs=[
            pl.BlockSpec((tq, LANES), lambda h, i: (i, h)),
            pl.BlockSpec((s, LANES), lambda h, i: (0, h)),
            pl.BlockSpec((s, LANES), lambda h, i: (0, h)),
        ],
        out_specs=pl.BlockSpec((tq, LANES), lambda h, i: (i, h)),
        out_shape=jax.ShapeDtypeStruct((s, d_attn), F32),
        scratch_shapes=[pltpu.VMEM((tq, LANES), F32)],
        compiler_params=pltpu.CompilerParams(
            dimension_semantics=("arbitrary", "arbitrary"), vmem_limit_bytes=VMEM_LIMIT_BYTES),
        name="attn",
    )(q, k, v)


def _out_proj_kernel(mp_ref, at_ref, x_ref, wout_ref, as_ref, gpost_ref, gffn_ref,
                     x1_ref, h2_ref, *, d_pool):
    at = at_ref[...]
    an = (at * _rms_scale(at) * as_ref[...]).astype(BF16)
    mix = _dot(mp_ref[...], wout_ref[0:d_pool, :]) + _dot(an, wout_ref[d_pool:, :])
    x1 = x_ref[...] + mix * _rms_scale(mix) * gpost_ref[...]
    x1_ref[...] = x1
    h2_ref[...] = (x1 * _rms_scale(x1) * gffn_ref[...]).astype(BF16)


def _out_proj(mp, attn, x, w_out, attn_scale, g_post, g_ffn, *, tm):
    s, d = x.shape
    d_pool = mp.shape[1]
    d_attn = attn.shape[1]
    row = lambda i: (i, 0)
    const2 = lambda i: (0, 0)
    kern = functools.partial(_out_proj_kernel, d_pool=d_pool)
    return pl.pallas_call(
        kern,
        grid=(s // tm,),
        in_specs=[
            pl.BlockSpec((tm, d_pool), row),
            pl.BlockSpec((tm, d_attn), row),
            pl.BlockSpec((tm, d), row),
            pl.BlockSpec(w_out.shape, const2),
            pl.BlockSpec((1, d_attn), const2),
            pl.BlockSpec((1, d), const2),
            pl.BlockSpec((1, d), const2),
        ],
        out_specs=[pl.BlockSpec((tm, d), row), pl.BlockSpec((tm, d), row)],
        out_shape=[jax.ShapeDtypeStruct((s, d), F32), jax.ShapeDtypeStruct((s, d), BF16)],
        compiler_params=pltpu.CompilerParams(
            dimension_semantics=("arbitrary",), vmem_limit_bytes=VMEM_LIMIT_BYTES),
        name="out_proj",
    )(mp, attn, x, w_out, attn_scale, g_post, g_ffn)


def _causal_conv(raw, prev, cw, cb):
    tm = raw.shape[0]
    row8 = lax.broadcasted_iota(jnp.int32, (SUBLANES, raw.shape[1]), 0)

    def shifted(sh):
        r = pltpu.roll(raw, sh, axis=0)
        head = jnp.where(row8 < sh, pltpu.roll(prev, sh, axis=0), r[0:SUBLANES, :])
        return jnp.concatenate([head, r[SUBLANES:, :]], axis=0)

    return cb + cw[0:1, :] * shifted(2) + cw[1:2, :] * shifted(1) + cw[2:3, :] * raw


def _ffn_kernel(h2_ref, x1_ref, wg_ref, wv_ref, cwg_ref, cwv_ref, cbg_ref, cbv_ref, wd_ref, gpost_ref,
                o_ref, acc_ref, carry_ref, *, tm, fc):
    i = pl.program_id(0)
    c = pl.program_id(1)
    nc = pl.num_programs(1)

    @pl.when(c == 0)
    def _():
        acc_ref[...] = jnp.zeros_like(acc_ref)

    h2 = h2_ref[...]
    g_raw = _dot(h2, wg_ref[...])
    v_raw = _dot(h2, wv_ref[...])
    prev = jnp.where(i == 0, 0.0, carry_ref[c])
    carry_ref[c] = jnp.concatenate([g_raw[tm - SUBLANES:, :], v_raw[tm - SUBLANES:, :]], axis=1)

    gate = _causal_conv(g_raw, prev[:, 0:fc], cwg_ref[...], cbg_ref[...])
    val = _causal_conv(v_raw, prev[:, fc:], cwv_ref[...], cbv_ref[...])
    act = gate * (1.0 / (1.0 + jnp.exp(-gate))) * val
    acc_ref[...] += _dot(act.astype(BF16), wd_ref[...])

    @pl.when(c == nc - 1)
    def _():
        f = acc_ref[...]
        o_ref[...] = x1_ref[...] + f * _rms_scale(f) * gpost_ref[...]


def _ffn(h2, x1, w_up, conv_w, conv_b, w_down, g_post, *, tm, fc):
    s, d = x1.shape
    d_ff = w_down.shape[0]
    nc = d_ff // fc
    kern = functools.partial(_ffn_kernel, tm=tm, fc=fc)
    return pl.pallas_call(
        kern,
        grid=(s // tm, nc),
        in_specs=[
            pl.BlockSpec((tm, d), lambda i, c: (i, 0)),
            pl.BlockSpec((tm, d), lambda i, c: (i, 0)),
            pl.BlockSpec((d, fc), lambda i, c: (0, c)),
            pl.BlockSpec((d, fc), lambda i, c: (0, c + nc)),
            pl.BlockSpec((CONV_WIDTH, fc), lambda i, c: (0, c)),
            pl.BlockSpec((CONV_WIDTH, fc), lambda i, c: (0, c + nc)),
            pl.BlockSpec((1, fc), lambda i, c: (0, c)),
            pl.BlockSpec((1, fc), lambda i, c: (0, c + nc)),
            pl.BlockSpec((fc, d), lambda i, c: (c, 0)),
            pl.BlockSpec((1, d), lambda i, c: (0, 0)),
        ],
        out_specs=pl.BlockSpec((tm, d), lambda i, c: (i, 0)),
        out_shape=jax.ShapeDtypeStruct((s, d), F32),
        scratch_shapes=[
            pltpu.VMEM((tm, d), F32),
            pltpu.VMEM((nc, SUBLANES, 2 * fc), F32),
        ],
        compiler_params=pltpu.CompilerParams(
            dimension_semantics=("arbitrary", "arbitrary"), vmem_limit_bytes=VMEM_LIMIT_BYTES),
        name="ffn",
    )(h2, x1, w_up, w_up, conv_w, conv_w, conv_b, conv_b, w_down, g_post)


def _pick_tile(s, pref):
    t = min(pref, s)
    assert s % t == 0, (s, t)
    return t


def kernel(x, norm_mix_pre, w_in, w_pool, pool_scale, attn_scale, w_out, norm_mix_post, norm_ffn_pre,
           w_up, conv_w, conv_b, w_down, norm_ffn_post):
    b, s, d = x.shape
    depth = w_in.shape[0]
    assert b == 1
    d_ff = w_down.shape[1]
    fc = 2 * LANES
    assert d_ff % fc == 0 and w_up.shape[2] == 2 * d_ff
    xs = x[0]
    for l in range(depth):
        row1 = lambda a: a[l][None, :]
        mp, q, k, v = _in_proj(xs, row1(norm_mix_pre), w_in[l].astype(BF16), w_pool[l].astype(BF16),
                               row1(pool_scale), tm=_pick_tile(s, 512))
        attn = _attention(q, k, v, tq=_pick_tile(s, 256))
        x1, h2 = _out_proj(mp, attn, xs, w_out[l].astype(BF16), row1(attn_scale), row1(norm_mix_post),
                           row1(norm_ffn_pre), tm=_pick_tile(s, 512))
        xs = _ffn(h2, x1, w_up[l].astype(BF16), conv_w[l], row1(conv_b), w_down[l].astype(BF16),
                  row1(norm_ffn_post), tm=_pick_tile(s, 1024), fc=fc)
    return xs[None]
```

```python
import functools

import jax
import jax.numpy as jnp
from jax import lax
from jax.experimental import pallas as pl
from jax.experimental.pallas import tpu as pltpu

EPS = 1e-6
POOL_WINDOWS = (2, 4, 8, 16)
HEAD_DIM = 64
CONV_WIDTH = 3
LANES = 128
SUBLANES = 8
POOL_HALO = 16
EXP2_UNDERFLOW = 152.0
LOG2_E = 1.4426950408889634
VMEM_LIMIT_BYTES = 56 * 1024 * 1024

BF16 = jnp.bfloat16
F32 = jnp.float32


def _rms_scale(x):
    return lax.rsqrt(jnp.mean(x * x, axis=-1, keepdims=True) + EPS)


def _dot(a, b):
    return jnp.dot(a, b, preferred_element_type=F32)


def _in_proj_kernel(x_ref, g_ref, win_ref, wpool_ref, ps_ref,
                    mp_ref, q_ref, k_ref, v_ref, halo_ref, *, tm, d_pool, d_attn):
    i = pl.program_id(0)

    @pl.when(i == 0)
    def _():
        halo_ref[...] = jnp.zeros_like(halo_ref)

    x = x_ref[...]
    hb = (x * _rms_scale(x) * g_ref[...]).astype(BF16)

    u = _dot(hb, win_ref[:, 0:d_pool])
    ext = jnp.concatenate([halo_ref[...], u], axis=0)
    halo_ref[...] = u[tm - POOL_HALO:, :]

    t1 = i * tm + lax.broadcasted_iota(jnp.int32, (tm, 1), 0) + 1
    gd = d_pool // len(POOL_WINDOWS)
    ys = []
    for g, w in enumerate(POOL_WINDOWS):
        s = ext[:, g * gd:(g + 1) * gd]
        sh = 1
        while sh < w:
            s = s + pltpu.roll(s, sh, axis=0)
            sh *= 2
        inv_cnt = 1.0 / jnp.minimum(t1, w).astype(F32)
        p = s[POOL_HALO:, :] * inv_cnt - u[:, g * gd:(g + 1) * gd]
        ys.append(_dot(p.astype(BF16), wpool_ref[g]))
    y = jnp.concatenate(ys, axis=1)
    mp_ref[...] = (y * _rms_scale(y) * ps_ref[...]).astype(BF16)

    scale = HEAD_DIM ** -0.5 * LOG2_E
    q_ref[...] = (_dot(hb, win_ref[:, d_pool:d_pool + d_attn]) * scale).astype(BF16)
    k_ref[...] = _dot(hb, win_ref[:, d_pool + d_attn:d_pool + 2 * d_attn]).astype(BF16)
    v_ref[...] = _dot(hb, win_ref[:, d_pool + 2 * d_attn:d_pool + 3 * d_attn]).astype(BF16)


def _in_proj(x, g_pre, w_in, w_pool, pool_scale, *, tm):
    s, d = x.shape
    d_pool = pool_scale.shape[-1]
    d_attn = (w_in.shape[1] - d_pool) // 3
    row = lambda i: (i, 0)
    const2 = lambda i: (0, 0)
    kern = functools.partial(_in_proj_kernel, tm=tm, d_pool=d_pool, d_attn=d_attn)
    return pl.pallas_call(
        kern,
        grid=(s // tm,),
        in_specs=[
            pl.BlockSpec((tm, d), row),
            pl.BlockSpec((1, d), const2),
            pl.BlockSpec(w_in.shape, const2),
            pl.BlockSpec(w_pool.shape, lambda i: (0, 0, 0)),
            pl.BlockSpec((1, d_pool), const2),
        ],
        out_specs=[
            pl.BlockSpec((tm, d_pool), row),
            pl.BlockSpec((tm, d_attn), row),
            pl.BlockSpec((tm, d_attn), row),
            pl.BlockSpec((tm, d_attn), row),
        ],
        out_shape=[
            jax.ShapeDtypeStruct((s, d_pool), BF16),
            jax.ShapeDtypeStruct((s, d_attn), BF16),
            jax.ShapeDtypeStruct((s, d_attn), BF16),
            jax.ShapeDtypeStruct((s, d_attn), BF16),
        ],
        scratch_shapes=[pltpu.VMEM((POOL_HALO, d_pool), F32)],
        compiler_params=pltpu.CompilerParams(
            dimension_semantics=("arbitrary",), vmem_limit_bytes=VMEM_LIMIT_BYTES),
        name="in_proj",
    )(x, g_pre, w_in, w_pool, pool_scale)


def _attn_kernel(q_ref, k_ref, v_ref, o_ref, *, tb):
    qi = pl.program_id(1)
    q = q_ref[...]
    lane_q = lax.broadcasted_iota(jnp.int32, q.shape, 1)
    zero_q = jnp.zeros_like(q)
    q_heads = (jnp.where(lane_q < HEAD_DIM, q, zero_q), jnp.where(lane_q >= HEAD_DIM, q, zero_q))

    r_i = lax.broadcasted_iota(jnp.int32, (tb, tb), 0)
    c_i = lax.broadcasted_iota(jnp.int32, (tb, tb), 1)
    tri = jnp.where(r_i >= c_i, 1.0, 0.0).astype(BF16)
    tri2 = jnp.concatenate([tri, tri], axis=0)
    causal = c_i < r_i
    lane_v = lax.broadcasted_iota(jnp.int32, (tb, LANES), 1)
    lo_rows = slice(0, tb)
    hi_rows = slice(tb, 2 * tb)

    def key_block(j):
        start = pl.multiple_of(j * tb, tb)
        vb = v_ref[pl.ds(start, tb), :]
        zero_v = jnp.zeros_like(vb)
        v_heads = (jnp.where(lane_v < HEAD_DIM, vb, zero_v), jnp.where(lane_v >= HEAD_DIM, vb, zero_v))
        return k_ref[pl.ds(start, tb), :], v_heads

    def logits(qh, kb):
        return lax.dot_general(qh, kb, (((1,), (1,)), ((), ())), preferred_element_type=F32)

    def weights(z, carry, masked=False, valid=None):
        sp = jnp.maximum(z, 0.0) + jnp.log2(1.0 + jnp.exp2(-jnp.abs(z)))
        if masked:
            sp = jnp.where(causal, sp, 0.0)
        if valid is not None:
            sp = jnp.where(valid, sp, 0.0)
        hi = sp.astype(BF16)
        lo = (sp - hi.astype(F32)).astype(BF16)
        cs = _dot(jnp.concatenate([hi, lo], axis=1), tri2)
        a = jnp.exp2(z - cs - carry)
        if masked:
            a = jnp.where(causal, a, 0.0)
        if valid is not None:
            a = jnp.where(valid, a, 0.0)
        return a.astype(BF16), carry + jnp.sum(sp, axis=-1, keepdims=True)

    k_hi, v_hi = key_block(2 * qi + 1)
    k_lo, v_lo = key_block(2 * qi)
    k_pre, v_pre = key_block(jnp.maximum(2 * qi - 1, 0))
    has_pre = qi > 0
    zero_c = jnp.zeros((tb, 1), F32)
    a_hi, a_lo, v_for_hi, v_for_lo, c_hi, c_lo = [], [], [], [], [], []
    for h in range(2):
        z_mid = logits(q_heads[h], k_lo)
        a1, c = weights(logits(q_heads[h][hi_rows], k_hi), zero_c, masked=True)
        a2, c = weights(z_mid[hi_rows], c)
        a_hi += [a1, a2]
        v_for_hi += [v_hi[h], v_lo[h]]
        c_hi.append(c)
        a1, c = weights(z_mid[lo_rows], zero_c, masked=True)
        a2, c = weights(logits(q_heads[h][lo_rows], k_pre), c, valid=has_pre)
        a_lo += [a1, a2]
        v_for_lo += [v_lo[h], v_pre[h]]
        c_lo.append(c)
    o_ref[hi_rows, :] = _dot(jnp.concatenate(a_hi, axis=1), jnp.concatenate(v_for_hi, axis=0))
    o_ref[lo_rows, :] = _dot(jnp.concatenate(a_lo, axis=1), jnp.concatenate(v_for_lo, axis=0))

    def sweep(rows, j_first, carries):
        def unfinished(state):
            j, c0, c1 = state
            return jnp.logical_and(j >= 0, jnp.min(jnp.minimum(c0, c1)) < EXP2_UNDERFLOW)

        def step(state):
            j, c0, c1 = state
            kb, v_heads = key_block(j)
            a0, c0 = weights(logits(q_heads[0][rows], kb), c0)
            a1, c1 = weights(logits(q_heads[1][rows], kb), c1)
            o_ref[rows, :] += _dot(jnp.concatenate([a0, a1], axis=1), jnp.concatenate(v_heads, axis=0))
            return j - 1, c0, c1

        lax.while_loop(unfinished, step, (j_first, carries[0], carries[1]))

    sweep(hi_rows, 2 * qi - 1, c_hi)
    sweep(lo_rows, 2 * qi - 2, c_lo)


def _attention(q, k, v, *, tb):
    s, d_attn = q.shape
    n_pairs = d_attn // LANES
    tq = 2 * tb
    kern = functools.partial(_attn_kernel, tb=tb)
    return pl.pallas_call(
        kern,
        grid=(n_pairs, s // tq),
        in_specs=[
            pl.BlockSpec((tq, LANES), lambda h, i: (i, h)),
            pl.BlockSpec((s, LANES), lambda h, i: (0, h)),
            pl.BlockSpec((s, LANES), lambda h, i: (0, h)),
        ],
        out_specs=pl.BlockSpec((tq, LANES), lambda h, i: (i, h)),
        out_shape=jax.ShapeDtypeStruct((s, d_attn), F32),
        compiler_params=pltpu.CompilerParams(
            dimension_semantics=("arbitrary", "arbitrary"), vmem_limit_bytes=VMEM_LIMIT_BYTES),
        name="attn",
    )(q, k, v)


def _out_proj_kernel(mp_ref, at_ref, x_ref, wout_ref, as_ref, gpost_ref, gffn_ref,
                     x1_ref, h2_ref, *, d_pool):
    at = at_ref[...]
    an = (at * _rms_scale(at) * as_ref[...]).astype(BF16)
    mix = _dot(mp_ref[...], wout_ref[0:d_pool, :]) + _dot(an, wout_ref[d_pool:, :])
    x1 = x_ref[...] + mix * _rms_scale(mix) * gpost_ref[...]
    x1_ref[...] = x1
    h2_ref[...] = (x1 * _rms_scale(x1) * gffn_ref[...]).astype(BF16)


def _out_proj(mp, attn, x, w_out, attn_scale, g_post, g_ffn, *, tm):
    s, d = x.shape
    d_pool = mp.shape[1]
    d_attn = attn.shape[1]
    row = lambda i: (i, 0)
    const2 = lambda i: (0, 0)
    kern = functools.partial(_out_proj_kernel, d_pool=d_pool)
    return pl.pallas_call(
        kern,
        grid=(s // tm,),
        in_specs=[
            pl.BlockSpec((tm, d_pool), row),
            pl.BlockSpec((tm, d_attn), row),
            pl.BlockSpec((tm, d), row),
            pl.BlockSpec(w_out.shape, const2),
            pl.BlockSpec((1, d_attn), const2),
            pl.BlockSpec((1, d), const2),
            pl.BlockSpec((1, d), const2),
        ],
        out_specs=[pl.BlockSpec((tm, d), row), pl.BlockSpec((tm, d), row)],
        out_shape=[jax.ShapeDtypeStruct((s, d), F32), jax.ShapeDtypeStruct((s, d), BF16)],
        compiler_params=pltpu.CompilerParams(
            dimension_semantics=("arbitrary",), vmem_limit_bytes=VMEM_LIMIT_BYTES),
        name="out_proj",
    )(mp, attn, x, w_out, attn_scale, g_post, g_ffn)


def _causal_conv(raw, prev, cw, cb):
    tm = raw.shape[0]
    row8 = lax.broadcasted_iota(jnp.int32, (SUBLANES, raw.shape[1]), 0)

    def shifted(sh):
        r = pltpu.roll(raw, sh, axis=0)
        head = jnp.where(row8 < sh, pltpu.roll(prev, sh, axis=0), r[0:SUBLANES, :])
        return jnp.concatenate([head, r[SUBLANES:, :]], axis=0)

    return cb + cw[0:1, :] * shifted(2) + cw[1:2, :] * shifted(1) + cw[2:3, :] * raw


def _ffn_kernel(h2_ref, x1_ref, wg_ref, wv_ref, cwg_ref, cwv_ref, cbg_ref, cbv_ref, wd_ref, gpost_ref,
                o_ref, acc_ref, carry_ref, *, tm, fc):
    i = pl.program_id(0)
    c = pl.program_id(1)
    nc = pl.num_programs(1)

    @pl.when(c == 0)
    def _():
        acc_ref[...] = jnp.zeros_like(acc_ref)

    h2 = h2_ref[...]
    g_raw = _dot(h2, wg_ref[...])
    v_raw = _dot(h2, wv_ref[...])
    prev = jnp.where(i == 0, 0.0, carry_ref[c])
    carry_ref[c] = jnp.concatenate([g_raw[tm - SUBLANES:, :], v_raw[tm - SUBLANES:, :]], axis=1)

    gate = _causal_conv(g_raw, prev[:, 0:fc], cwg_ref[...], cbg_ref[...])
    val = _causal_conv(v_raw, prev[:, fc:], cwv_ref[...], cbv_ref[...])
    act = gate * (1.0 / (1.0 + jnp.exp(-gate))) * val
    acc_ref[...] += _dot(act.astype(BF16), wd_ref[...])

    @pl.when(c == nc - 1)
    def _():
        f = acc_ref[...]
        o_ref[...] = x1_ref[...] + f * _rms_scale(f) * gpost_ref[...]


def _ffn(h2, x1, w_up, conv_w, conv_b, w_down, g_post, *, tm, fc):
    s, d = x1.shape
    d_ff = w_down.shape[0]
    nc = d_ff // fc
    kern = functools.partial(_ffn_kernel, tm=tm, fc=fc)
    return pl.pallas_call(
        kern,
        grid=(s // tm, nc),
        in_specs=[
            pl.BlockSpec((tm, d), lambda i, c: (i, 0)),
            pl.BlockSpec((tm, d), lambda i, c: (i, 0)),
            pl.BlockSpec((d, fc), lambda i, c: (0, c)),
            pl.BlockSpec((d, fc), lambda i, c: (0, c + nc)),
            pl.BlockSpec((CONV_WIDTH, fc), lambda i, c: (0, c)),
            pl.BlockSpec((CONV_WIDTH, fc), lambda i, c: (0, c + nc)),
            pl.BlockSpec((1, fc), lambda i, c: (0, c)),
            pl.BlockSpec((1, fc), lambda i, c: (0, c + nc)),
            pl.BlockSpec((fc, d), lambda i, c: (c, 0)),
            pl.BlockSpec((1, d), lambda i, c: (0, 0)),
        ],
        out_specs=pl.BlockSpec((tm, d), lambda i, c: (i, 0)),
        out_shape=jax.ShapeDtypeStruct((s, d), F32),
        scratch_shapes=[
            pltpu.VMEM((tm, d), F32),
            pltpu.VMEM((nc, SUBLANES, 2 * fc), F32),
        ],
        compiler_params=pltpu.CompilerParams(
            dimension_semantics=("arbitrary", "arbitrary"), vmem_limit_bytes=VMEM_LIMIT_BYTES),
        name="ffn",
    )(h2, x1, w_up, w_up, conv_w, conv_w, conv_b, conv_b, w_down, g_post)


def _pick_tile(s, pref):
    t = min(pref, s)
    assert s % t == 0, (s, t)
    return t


def kernel(x, norm_mix_pre, w_in, w_pool, pool_scale, attn_scale, w_out, norm_mix_post, norm_ffn_pre,
           w_up, conv_w, conv_b, w_down, norm_ffn_post):
    b, s, d = x.shape
    depth = w_in.shape[0]
    assert b == 1
    d_ff = w_down.shape[1]
    fc = 2 * LANES
    assert d_ff % fc == 0 and w_up.shape[2] == 2 * d_ff
    xs = x[0]
    for l in range(depth):
        row1 = lambda a: a[l][None, :]
        mp, q, k, v = _in_proj(xs, row1(norm_mix_pre), w_in[l].astype(BF16), w_pool[l].astype(BF16),
                               row1(pool_scale), tm=_pick_tile(s, 512))
        attn = _attention(q, k, v, tb=_pick_tile(s, 512) // 2)
        x1, h2 = _out_proj(mp, attn, xs, w_out[l].astype(BF16), row1(attn_scale), row1(norm_mix_post),
                           row1(norm_ffn_pre), tm=_pick_tile(s, 512))
        xs = _ffn(h2, x1, w_up[l].astype(BF16), conv_w[l], row1(conv_b), w_down[l].astype(BF16),
                  row1(norm_ffn_post), tm=_pick_tile(s, 1024), fc=fc)
    return xs[None]
```
